```python
import math
import jax, jax.numpy as jnp
from jax import lax
import numpy as np

D_MODEL = 2048
BATCH = 4
SEQ = 2048
DEPTH = 2
DEC_BATCH = 128
DEC_SEQ = 8
PAST_LEN = 16384
PAGE_SIZE = 128

M_WIDTH = D_MODEL // 2
C_WIDTH = D_MODEL - M_WIDTH
N_HEADS = 4
HEAD_DIM = M_WIDTH // N_HEADS
CONV_K = 31
CHUNK = 64
LN_EPS = 1e-5
DN_ALPHA = (2 * DEPTH) ** 0.25
DN_BETA = (8 * DEPTH) ** -0.25
SPLITS = (M_WIDTH, 2 * M_WIDTH, 3 * M_WIDTH, 4 * M_WIDTH, 5 * M_WIDTH,
          5 * M_WIDTH + C_WIDTH, 5 * M_WIDTH + 2 * C_WIDTH, 5 * M_WIDTH + 3 * C_WIDTH,
          5 * M_WIDTH + 3 * C_WIDTH + N_HEADS)
F_OFF = 5 * M_WIDTH + 3 * C_WIDTH + N_HEADS
PROJ_COLS = 5 * M_WIDTH + 3 * C_WIDTH + 2 * N_HEADS

kernel_name = "hybrid_mlstm_conformer_conv_decode_step"


def _layernorm(x):
    x = x.astype(jnp.float32)
    mu = jnp.mean(x, axis=-1, keepdims=True)
    var = jnp.mean(jnp.square(x - mu), axis=-1, keepdims=True)
    return (x - mu) * lax.rsqrt(var + LN_EPS)


def _mlstm(q, k, v, ig, fg, C0, n0, m0):
    B, S, H, DH = q.shape
    L = math.gcd(S, CHUNK)
    NC = S // L
    to_chunks = lambda t: t.reshape(B, NC, L, H, DH).transpose(1, 0, 3, 2, 4)
    gate_chunks = lambda t: t.reshape(B, NC, L, H).transpose(1, 0, 3, 2)
    logf = jax.nn.log_sigmoid(fg)
    causal = jnp.tril(jnp.ones((L, L), dtype=bool))

    def step(carry, inp):
        C, n, m = carry
        qc, kc, vc, ic, lf = inp
        b = jnp.cumsum(lf, axis=-1)
        dmat = b[..., :, None] - b[..., None, :] + ic[..., None, :]
        dmat = jnp.where(causal, dmat, -jnp.inf)
        inter = b + m[..., None]
        m_t = jnp.maximum(inter, jnp.max(dmat, axis=-1))
        w_inter = jnp.exp(inter - m_t)
        s = jnp.einsum('bhtd,bhsd->bhts', qc, kc) * jnp.exp(dmat - m_t[..., None])
        num = (w_inter[..., None] * jnp.einsum('bhtd,bhde->bhte', qc, C)
               + jnp.einsum('bhts,bhse->bhte', s, vc))
        den = w_inter * jnp.einsum('bhtd,bhd->bht', qc, n) + jnp.sum(s, axis=-1)
        h = num / jnp.maximum(jnp.abs(den), jnp.exp(-m_t))[..., None]
        m_new = m_t[..., -1]
        decay = jnp.exp(b[..., -1] + m - m_new)
        ws = jnp.exp(b[..., -1:] - b + ic - m_new[..., None])
        C_new = decay[..., None, None] * C + jnp.einsum('bhs,bhsd,bhse->bhde', ws, kc, vc)
        n_new = decay[..., None] * n + jnp.einsum('bhs,bhsd->bhd', ws, kc)
        return (C_new, n_new, m_new), h

    (C1, n1, m1), h = lax.scan(step, (C0, n0, m0),
                               (to_chunks(q), to_chunks(k), to_chunks(v),
                                gate_chunks(ig), gate_chunks(logf)))
    h = h.transpose(1, 0, 3, 2, 4).reshape(B, S, H, DH)
    return h, C1, n1, m1


def _conv_module(ga, gg, buf, w_dw, b_dw, cln_g, cln_b):
    u = ga.astype(jnp.float32) * jax.nn.sigmoid(gg.astype(jnp.float32))
    full = jnp.concatenate([buf.astype(jnp.float32), u], axis=1)
    y = lax.conv_general_dilated(full, w_dw.astype(jnp.float32)[:, None, :],
                                 window_strides=(1,), padding='VALID',
                                 dimension_numbers=('NWC', 'WIO', 'NWC'),
                                 feature_group_count=C_WIDTH)
    y = y + b_dw.astype(jnp.float32)
    y = _layernorm(y) * cln_g.astype(jnp.float32) + cln_b.astype(jnp.float32)
    return jax.nn.silu(y), full[:, -(CONV_K - 1):, :]


def _layer(x, C0, n0, m0, buf, w_in, b_in, hn_g, w_dw, b_dw, cln_g, cln_b, w_out, ln_g, ln_b):
    B, S, _ = x.shape
    proj = jnp.einsum('bsd,de->bse', x, w_in) + b_in
    q, k, v, o, zm, ga, gg, zc, ig, fg = jnp.split(proj, SPLITS, axis=-1)
    heads = lambda t: t.astype(jnp.float32).reshape(B, S, N_HEADS, HEAD_DIM)
    h, C1, n1, m1 = _mlstm(heads(q), heads(k) * (HEAD_DIM ** -0.5), heads(v),
                           ig.astype(jnp.float32), fg.astype(jnp.float32), C0, n0, m0)
    h = (_layernorm(h) * hn_g.astype(jnp.float32).reshape(N_HEADS, HEAD_DIM)).reshape(B, S, M_WIDTH)
    mix_m = h * jax.nn.sigmoid(o.astype(jnp.float32)) * jax.nn.silu(zm.astype(jnp.float32))
    u, buf1 = _conv_module(ga, gg, buf, w_dw, b_dw, cln_g, cln_b)
    mix_c = u * jax.nn.silu(zc.astype(jnp.float32))
    mix = jnp.concatenate([mix_m, mix_c], axis=-1).astype(x.dtype)
    out = jnp.einsum('bse,ed->bsd', mix, w_out)
    y = _layernorm(DN_ALPHA * x.astype(jnp.float32) + out.astype(jnp.float32))
    y = y * ln_g.astype(jnp.float32) + ln_b.astype(jnp.float32)
    return y.astype(x.dtype), C1, n1, m1, buf1


def setup_inputs(seed: int = 0) -> dict:
    key = jax.random.key(seed)
    ks = jax.random.split(key, 20)
    f32 = jnp.float32
    nrm = lambda k, shape, s: s * jax.random.normal(k, shape, f32)
    b_in = nrm(ks[7], (DEPTH, PROJ_COLS), 0.02)
    b_in = b_in.at[:, F_OFF:].add(jnp.linspace(3.0, 6.0, N_HEADS, dtype=f32))
    return {
        "x_prompt": nrm(ks[0], (BATCH, SEQ, D_MODEL), 1.0),
        "x_sample": nrm(ks[1], (DEC_BATCH, DEC_SEQ, D_MODEL), 1.0),
        "state_C": nrm(ks[2], (DEPTH, DEC_BATCH, N_HEADS, HEAD_DIM, HEAD_DIM), 0.1),
        "state_n": nrm(ks[3], (DEPTH, DEC_BATCH, N_HEADS, HEAD_DIM), 0.1),
        "state_m": nrm(ks[4], (DEPTH, DEC_BATCH, N_HEADS), 1.0),
        "state_conv": nrm(ks[5], (DEPTH, DEC_BATCH, CONV_K - 1, C_WIDTH), 0.5),
        "w_in": nrm(ks[6], (DEPTH, D_MODEL, PROJ_COLS), D_MODEL ** -0.5),
        "b_in": b_in,
        "hn_g": 1.0 + nrm(ks[8], (DEPTH, M_WIDTH), 0.02),
        "w_dw": nrm(ks[9], (DEPTH, CONV_K, C_WIDTH), CONV_K ** -0.5),
        "b_dw": nrm(ks[10], (DEPTH, C_WIDTH), 0.02),
        "cln_g": 1.0 + nrm(ks[11], (DEPTH, C_WIDTH), 0.02),
        "cln_b": nrm(ks[12], (DEPTH, C_WIDTH), 0.02),
        "w_out": nrm(ks[13], (DEPTH, D_MODEL, D_MODEL), DN_BETA * D_MODEL ** -0.5),
        "ln_g": 1.0 + nrm(ks[14], (DEPTH, D_MODEL), 0.02),
        "ln_b": nrm(ks[15], (DEPTH, D_MODEL), 0.02),
    }


def reference(x_prompt, x_sample, state_C, state_n, state_m, state_conv,
              w_in, b_in, hn_g, w_dw, b_dw, cln_g, cln_b, w_out, ln_g, ln_b):
    Bp = x_prompt.shape[0]
    f32 = jnp.float32
    yp, ys = x_prompt, x_sample
    Cp_l, np_l, mp_l, cp_l = [], [], [], []
    Cs_l, ns_l, ms_l, cs_l = [], [], [], []
    for l in range(DEPTH):
        weights = (w_in[l], b_in[l], hn_g[l], w_dw[l], b_dw[l], cln_g[l], cln_b[l],
                   w_out[l], ln_g[l], ln_b[l])
        C0 = jnp.zeros((Bp, N_HEADS, HEAD_DIM, HEAD_DIM), f32)
        n0 = jnp.zeros((Bp, N_HEADS, HEAD_DIM), f32)
        m0 = jnp.zeros((Bp, N_HEADS), f32)
        buf0 = jnp.zeros((Bp, CONV_K - 1, C_WIDTH), f32)
        yp, Cp, npr, mp, cp = _layer(yp, C0, n0, m0, buf0, *weights)
        ys, Cs, ns, ms, cs = _layer(ys, state_C[l].astype(f32), state_n[l].astype(f32),
                                    state_m[l].astype(f32), state_conv[l], *weights)
        Cp_l.append(Cp); np_l.append(npr); mp_l.append(mp); cp_l.append(cp)
        Cs_l.append(Cs); ns_l.append(ns); ms_l.append(ms); cs_l.append(cs)
    return (yp, ys,
            jnp.stack(Cp_l), jnp.stack(np_l), jnp.stack(mp_l), jnp.stack(cp_l),
            jnp.stack(Cs_l), jnp.stack(ns_l), jnp.stack(ms_l), jnp.stack(cs_l))
```

```python
import functools

import jax
import jax.numpy as jnp
from jax import lax
from jax.experimental import pallas as pl
from jax.experimental.pallas import tpu as pltpu

D_MODEL = 2048
DEPTH = 2
M_WIDTH = D_MODEL // 2
C_WIDTH = D_MODEL - M_WIDTH
N_HEADS = 4
HEAD_DIM = M_WIDTH // N_HEADS
CONV_K = 31
LN_EPS = 1e-5
DN_ALPHA = (2 * DEPTH) ** 0.25
MAIN_COLS = 5 * M_WIDTH + 3 * C_WIDTH
K_SCALE = HEAD_DIM ** -0.5

LANES = 128
CONV_HALO = 32
VMEM_LIMIT = 56 * 1024 * 1024

PROJ_TM = 1024
PROJ_TN = 512
CHUNK = 256
SAMPLE_G = 4
SAMPLE_LK = 128
CONV_TT = 64
CONV_LB = 256
OUT_TM = 256

_BF = jnp.bfloat16
_F32 = jnp.float32


def _params(sem):
    return pltpu.CompilerParams(dimension_semantics=sem, vmem_limit_bytes=VMEM_LIMIT)


def _sigmoid(x):
    return 1.0 / (1.0 + jnp.exp(-x))


def _log_sigmoid(x):
    return jnp.minimum(x, 0.0) - jnp.log1p(jnp.exp(-jnp.abs(x)))


def _in_proj_kernel(x_ref, w_ref, b_ref, wg_ref, bg_ref, proj_ref, gates_ref, xb_ref):
    @pl.when(pl.program_id(1) == 0)
    def _():
        xb_ref[...] = x_ref[...].astype(_BF)
        gates_ref[...] = jnp.dot(xb_ref[...], wg_ref[...], preferred_element_type=_F32) + bg_ref[...]

    proj_ref[...] = jnp.dot(xb_ref[...], w_ref[...], preferred_element_type=_F32) + b_ref[...]


def _in_proj(x, w_main, b_main, w_gate, b_gate, layer):
    T = x.shape[0]
    tm = min(PROJ_TM, T)
    grid = (T // tm, MAIN_COLS // PROJ_TN)
    return pl.pallas_call(
        _in_proj_kernel,
        grid=grid,
        in_specs=[
            pl.BlockSpec((tm, D_MODEL), lambda i, j: (i, 0)),
            pl.BlockSpec((None, D_MODEL, PROJ_TN), lambda i, j: (layer, 0, j)),
            pl.BlockSpec((None, 1, PROJ_TN), lambda i, j: (layer, 0, j)),
            pl.BlockSpec((None, D_MODEL, LANES), lambda i, j: (layer, 0, 0)),
            pl.BlockSpec((None, 1, LANES), lambda i, j: (layer, 0, 0)),
        ],
        out_specs=[
            pl.BlockSpec((tm, PROJ_TN), lambda i, j: (i, j)),
            pl.BlockSpec((tm, LANES), lambda i, j: (i, 0)),
        ],
        out_shape=[
            jax.ShapeDtypeStruct((T, MAIN_COLS), _F32),
            jax.ShapeDtypeStruct((T, LANES), _F32),
        ],
        scratch_shapes=[pltpu.VMEM((tm, D_MODEL), _BF)],
        compiler_params=_params(("arbitrary", "arbitrary")),
        name="in_proj",
    )(x, w_main, b_main, w_gate, b_gate)


def _seg_scan(x, seg, op, ident):
    pos = lax.broadcasted_iota(jnp.int32, x.shape, 0) & (seg - 1)
    s = 1
    while s < seg:
        shifted = pltpu.roll(x, s, axis=0)
        x = op(x, jnp.where(pos >= s, shifted, ident))
        s *= 2
    return x


def _pad_rows(x, rows):
    if x.shape[0] == rows:
        return x
    return jnp.concatenate([x, jnp.zeros((rows - x.shape[0], x.shape[1]), x.dtype)], axis=0)


def _mlstm_block(q_ref, k_ref, v_ref, o_ref, z_ref, g_ref, hn_ref, mix_ref,
                 c_src, n_src, m_src, c_dst, n_dst, m_dst, *, lq, lk, seg):
    groups = lq // seg
    g = g_ref[...]
    b_cum = _seg_scan(_log_sigmoid(g), seg, jnp.add, 0.0)
    b_al = pltpu.roll(b_cum, LANES - N_HEADS, axis=1)
    a = g - b_al
    a_max = _seg_scan(a, seg, jnp.maximum, -jnp.inf)
    a_t = _pad_rows(a, lk).T

    row = lax.broadcasted_iota(jnp.int32, (lq, lk), 0)
    col = lax.broadcasted_iota(jnp.int32, (lq, lk), 1)
    mask = col <= row
    if groups > 1:
        mask = mask & ((col // seg) == (row // seg))
    rid = lax.broadcasted_iota(jnp.int32, (lq, 1), 0) // seg
    cid = lax.broadcasted_iota(jnp.int32, (1, lk), 1) // seg

    def per_row(vals):
        out = vals[0]
        for s in range(1, groups):
            out = jnp.where(rid == s, vals[s], out)
        return out

    for h in range(N_HEADS):
        hs = slice(h * HEAD_DIM, (h + 1) * HEAD_DIM)
        a_col = a[:, h:h + 1]
        b_col = b_al[:, h:h + 1]
        a_row = a_t[h:h + 1, :]
        m_prev = [m_src[s, h:h + 1, 0:1] for s in range(groups)]
        m_prev_rows = per_row(m_prev)
        mx = jnp.maximum(a_max[:, h:h + 1], m_prev_rows)
        m_t = b_col + mx
        w_inter = jnp.exp(m_prev_rows - mx)

        q = q_ref[:, hs]
        k = k_ref[:, hs] * K_SCALE
        qb = q.astype(_BF)
        kb = _pad_rows(k.astype(_BF), lk)
        vb = _pad_rows(v_ref[:, hs].astype(_BF), lk)

        s_mat = lax.dot_general(qb, kb, (((1,), (1,)), ((), ())), preferred_element_type=_F32)
        s_mat = s_mat * jnp.exp(jnp.where(mask, a_row - mx, -jnp.inf))
        num = jnp.dot(s_mat.astype(_BF), vb, preferred_element_type=_F32)
        if groups == 1:
            inter = jnp.dot(qb, c_src[0, h].astype(_BF), preferred_element_type=_F32)
            qn = jnp.sum(q * n_src[0, h:h + 1, :], axis=1, keepdims=True)
        else:
            inter = jnp.zeros((lq, HEAD_DIM), _F32)
            for s in range(groups):
                q_s = jnp.where(rid == s, qb, jnp.zeros_like(qb))
                inter = inter + jnp.dot(q_s, c_src[s, h].astype(_BF), preferred_element_type=_F32)
            qn = jnp.sum(q * per_row([n_src[s, h:h + 1, :] for s in range(groups)]), axis=1, keepdims=True)
        num = w_inter * inter + num
        den = w_inter * qn + jnp.sum(s_mat, axis=1, keepdims=True)
        hh = num / jnp.maximum(jnp.abs(den), jnp.exp(-m_t))

        mu = jnp.mean(hh, axis=1, keepdims=True)
        hc = hh - mu
        var = jnp.mean(hc * hc, axis=1, keepdims=True)
        hn = hc * lax.rsqrt(var + LN_EPS) * hn_ref[:, hs]
        z = z_ref[:, hs]
        mix = hn * _sigmoid(o_ref[:, hs]) * (z * _sigmoid(z))
        mix_ref[:, hs] = mix.astype(mix_ref.dtype)

        last = [s * seg + seg - 1 for s in range(groups)]
        m_new = [m_t[r:r + 1, :] for r in last]
        b_end = [b_col[r:r + 1, :] for r in last]
        ws = jnp.exp(a_col + per_row([b_end[s] - m_new[s] for s in range(groups)]))
        kw = k * ws
        kw_t = _pad_rows(kw, lk).T
        for s in range(groups):
            decay = jnp.exp(b_end[s] + m_prev[s] - m_new[s])
            if groups == 1:
                kw_ts, kw_s = kw_t, kw
            else:
                kw_ts = jnp.where(cid == s, kw_t, 0.0)
                kw_s = jnp.where(rid == s, kw, 0.0)
            upd = jnp.dot(kw_ts.astype(_BF), vb, preferred_element_type=_F32)
            c_dst[s, h] = decay * c_src[s, h] + upd
            n_dst[s, h:h + 1, :] = decay * n_src[s, h:h + 1, :] + jnp.sum(kw_s, axis=0, keepdims=True)
            m_dst[s, h:h + 1, :] = jnp.broadcast_to(m_new[s], (1, LANES))


def _mlstm_prompt_kernel(q_ref, k_ref, v_ref, o_ref, z_ref, g_ref, hn_ref,
                         mix_ref, c_ref, n_ref, m_ref, *, chunk):
    @pl.when(pl.program_id(1) == 0)
    def _():
        c_ref[...] = jnp.zeros_like(c_ref)
        n_ref[...] = jnp.zeros_like(n_ref)
        m_ref[...] = jnp.zeros_like(m_ref)

    _mlstm_block(q_ref, k_ref, v_ref, o_ref, z_ref, g_ref, hn_ref, mix_ref,
                 c_ref, n_ref, m_ref, c_ref, n_ref, m_ref, lq=chunk, lk=chunk, seg=chunk)


def _mlstm_prompt(proj, gates, hn_g, layer, batch, seq):
    nc = seq // CHUNK
    col = lambda cb: pl.BlockSpec((CHUNK, M_WIDTH), lambda b, c: (b * nc + c, cb))
    return pl.pallas_call(
        functools.partial(_mlstm_prompt_kernel, chunk=CHUNK),
        grid=(batch, nc),
        in_specs=[col(0), col(1), col(2), col(3), col(4),
                  pl.BlockSpec((CHUNK, LANES), lambda b, c: (b * nc + c, 0)),
                  pl.BlockSpec((None, 1, M_WIDTH), lambda b, c: (layer, 0, 0))],
        out_specs=[
            pl.BlockSpec((CHUNK, M_WIDTH), lambda b, c: (b * nc + c, 0)),
            pl.BlockSpec((1, N_HEADS, HEAD_DIM, HEAD_DIM), lambda b, c: (b, 0, 0, 0)),
            pl.BlockSpec((1, N_HEADS, HEAD_DIM), lambda b, c: (b, 0, 0)),
            pl.BlockSpec((1, N_HEADS, LANES), lambda b, c: (b, 0, 0)),
        ],
        out_shape=[
            jax.ShapeDtypeStruct((batch * seq, M_WIDTH), _BF),
            jax.ShapeDtypeStruct((batch, N_HEADS, HEAD_DIM, HEAD_DIM), _F32),
            jax.ShapeDtypeStruct((batch, N_HEADS, HEAD_DIM), _F32),
            jax.ShapeDtypeStruct((batch, N_HEADS, LANES), _F32),
        ],
        compiler_params=_params(("arbitrary", "arbitrary")),
        name="mlstm_prompt",
    )(proj, proj, proj, proj, proj, gates, hn_g)


def _mlstm_sample_kernel(q_ref, k_ref, v_ref, o_ref, z_ref, g_ref, hn_ref, c_in, n_in, m_in, *rest, seg):
    mix_ref, c_out, n_out, m_out = rest[-4:]
    _mlstm_block(q_ref, k_ref, v_ref, o_ref, z_ref, g_ref, hn_ref, mix_ref,
                 c_in, n_in, m_in, c_out, n_out, m_out, lq=SAMPLE_G * seg, lk=SAMPLE_LK, seg=seg)


def _mlstm_sample(proj, gates, hn_g, state_c, state_n, m_lanes, c_all, layer, batch, seq):
    rows = SAMPLE_G * seq
    steps = batch // SAMPLE_G
    col = lambda cb: pl.BlockSpec((rows, M_WIDTH), lambda i: (i, cb))
    in_specs = [col(0), col(1), col(2), col(3), col(4),
                pl.BlockSpec((rows, LANES), lambda i: (i, 0)),
                pl.BlockSpec((None, 1, M_WIDTH), lambda i: (layer, 0, 0)),
                pl.BlockSpec((None, SAMPLE_G, N_HEADS, HEAD_DIM, HEAD_DIM), lambda i: (layer, i, 0, 0, 0)),
                pl.BlockSpec((None, SAMPLE_G, N_HEADS, HEAD_DIM), lambda i: (layer, i, 0, 0)),
                pl.BlockSpec((None, SAMPLE_G, N_HEADS, LANES), lambda i: (layer, i, 0, 0))]
    args = [proj, proj, proj, proj, proj, gates, hn_g, state_c, state_n, m_lanes]
    aliases = {}
    if c_all is not None:
        in_specs.append(pl.BlockSpec(memory_space=pl.ANY))
        args.append(c_all)
        aliases = {len(args) - 1: 1}
    return pl.pallas_call(
        functools.partial(_mlstm_sample_kernel, seg=seq),
        grid=(steps,),
        in_specs=in_specs,
        out_specs=[
            pl.BlockSpec((rows, M_WIDTH), lambda i: (i, 0)),
            pl.BlockSpec((None, SAMPLE_G, N_HEADS, HEAD_DIM, HEAD_DIM), lambda i: (layer, i, 0, 0, 0)),
            pl.BlockSpec((SAMPLE_G, N_HEADS, HEAD_DIM), lambda i: (i, 0, 0)),
            pl.BlockSpec((SAMPLE_G, N_HEADS, LANES), lambda i: (i, 0, 0)),
        ],
        out_shape=[
            jax.ShapeDtypeStruct((batch * seq, M_WIDTH), _BF),
            jax.ShapeDtypeStruct((DEPTH, batch, N_HEADS, HEAD_DIM, HEAD_DIM), _F32),
            jax.ShapeDtypeStruct((batch, N_HEADS, HEAD_DIM), _F32),
            jax.ShapeDtypeStruct((batch, N_HEADS, LANES), _F32),
        ],
        input_output_aliases=aliases,
        compiler_params=_params(("arbitrary",)),
        name="mlstm_sample",
    )(*args)


def _conv_kernel(ga_ref, gg_ref, zc_ref, buf_ref, w_ref, bdw_ref, cg_ref, cb_ref,
                 mix_ref, st_ref, full_ref, y_ref, *, tt, carry):
    first = CONV_HALO - (CONV_K - 1)

    @pl.when(pl.program_id(1) == 0)
    def _():
        full_ref[0:first, :] = jnp.zeros((first, C_WIDTH), _F32)
        full_ref[first:CONV_HALO, :] = buf_ref[...]

    if carry:
        @pl.when(pl.program_id(1) > 0)
        def _():
            full_ref[0:CONV_HALO, :] = full_ref[tt:tt + CONV_HALO, :]

    full_ref[CONV_HALO:CONV_HALO + tt, :] = ga_ref[...] * _sigmoid(gg_ref[...])

    for lb in range(C_WIDTH // CONV_LB):
        ls = slice(lb * CONV_LB, (lb + 1) * CONV_LB)
        acc = jnp.zeros((tt, CONV_LB), _F32)
        for j in range(CONV_K):
            acc = acc + full_ref[first + j:first + j + tt, ls] * w_ref[j:j + 1, ls]
        y_ref[:, ls] = acc + bdw_ref[:, ls]

    y = y_ref[...]
    mu = jnp.mean(y, axis=1, keepdims=True)
    yc = y - mu
    var = jnp.mean(yc * yc, axis=1, keepdims=True)
    yn = yc * lax.rsqrt(var + LN_EPS) * cg_ref[...] + cb_ref[...]
    zc = zc_ref[...]
    mix = (yn * _sigmoid(yn)) * (zc * _sigmoid(zc))
    mix_ref[...] = mix.astype(mix_ref.dtype)
    st_ref[...] = full_ref[tt + first:tt + CONV_HALO, :]


def _conv_module(proj, buf, buf_layer, w_dw, b_dw, cln_g, cln_b, st_all, layer, batch, seq, tt):
    nt = seq // tt
    first_col = 5 * M_WIDTH // C_WIDTH
    col = lambda cb: pl.BlockSpec((tt, C_WIDTH), lambda b, t: (b * nt + t, first_col + cb))
    vec = pl.BlockSpec((None, 1, C_WIDTH), lambda b, t: (layer, 0, 0))
    in_specs = [col(0), col(1), col(2),
                pl.BlockSpec((None, None, CONV_K - 1, C_WIDTH), lambda b, t: (buf_layer, b, 0, 0)),
                pl.BlockSpec((None, CONV_K, C_WIDTH), lambda b, t: (layer, 0, 0)),
                vec, vec, vec]
    args = [proj, proj, proj, buf, w_dw, b_dw, cln_g, cln_b]
    aliases = {}
    if st_all is not None:
        in_specs.append(pl.BlockSpec(memory_space=pl.ANY))
        args.append(st_all)
        aliases = {len(args) - 1: 1}
    return pl.pallas_call(
        functools.partial(_conv_kernel_entry, tt=tt, carry=nt > 1),
        grid=(batch, nt),
        in_specs=in_specs,
        out_specs=[
            pl.BlockSpec((tt, C_WIDTH), lambda b, t: (b * nt + t, 0)),
            pl.BlockSpec((None, None, CONV_K - 1, C_WIDTH), lambda b, t: (layer, b, 0, 0)),
        ],
        out_shape=[
            jax.ShapeDtypeStruct((batch * seq, C_WIDTH), _BF if tt % 16 == 0 else _F32),
            jax.ShapeDtypeStruct((DEPTH, batch, CONV_K - 1, C_WIDTH), _F32),
        ],
        scratch_shapes=[pltpu.VMEM((CONV_HALO + tt, C_WIDTH), _F32), pltpu.VMEM((tt, C_WIDTH), _F32)],
        input_output_aliases=aliases,
        compiler_params=_params(("arbitrary", "arbitrary")),
        name="conv_module",
    )(*args)


def _conv_kernel_entry(ga_ref, gg_ref, zc_ref, buf_ref, w_ref, bdw_ref, cg_ref, cb_ref, *rest, tt, carry):
    mix_ref, st_ref, full_ref, y_ref = rest[-4:]
    _conv_kernel(ga_ref, gg_ref, zc_ref, buf_ref, w_ref, bdw_ref, cg_ref, cb_ref,
                 mix_ref, st_ref, full_ref, y_ref, tt=tt, carry=carry)


def _out_proj_kernel(mm_ref, mc_ref, x_ref, w_ref, g_ref, b_ref, y_ref):
    acc = jnp.dot(mm_ref[...].astype(_BF), w_ref[0:M_WIDTH, :], preferred_element_type=_F32)
    acc = acc + jnp.dot(mc_ref[...].astype(_BF), w_ref[M_WIDTH:D_MODEL, :], preferred_element_type=_F32)
    z = DN_ALPHA * x_ref[...] + acc
    mu = jnp.mean(z, axis=1, keepdims=True)
    zc = z - mu
    var = jnp.mean(zc * zc, axis=1, keepdims=True)
    y_ref[...] = zc * lax.rsqrt(var + LN_EPS) * g_ref[...] + b_ref[...]


def _out_proj(mix_m, mix_c, x, w_out, ln_g, ln_b, layer):
    T = x.shape[0]
    tm = min(OUT_TM, T)
    vec = pl.BlockSpec((None, 1, D_MODEL), lambda i: (layer, 0, 0))
    return pl.pallas_call(
        _out_proj_kernel,
        grid=(T // tm,),
        in_specs=[
            pl.BlockSpec((tm, M_WIDTH), lambda i: (i, 0)),
            pl.BlockSpec((tm, C_WIDTH), lambda i: (i, 0)),
            pl.BlockSpec((tm, D_MODEL), lambda i: (i, 0)),
            pl.BlockSpec((None, D_MODEL, D_MODEL), lambda i: (layer, 0, 0)),
            vec, vec,
        ],
        out_specs=pl.BlockSpec((tm, D_MODEL), lambda i: (i, 0)),
        out_shape=jax.ShapeDtypeStruct((T, D_MODEL), _F32),
        compiler_params=_params(("arbitrary",)),
        name="out_proj",
    )(mix_m, mix_c, x, w_out, ln_g, ln_b)


def kernel(x_prompt, x_sample, state_C, state_n, state_m, state_conv,
           w_in, b_in, hn_g, w_dw, b_dw, cln_g, cln_b, w_out, ln_g, ln_b):
    bp, sp, _ = x_prompt.shape
    bs, ss, _ = x_sample.shape
    assert sp % CHUNK == 0 and sp % CONV_TT == 0 and bs % SAMPLE_G == 0
    assert ss & (ss - 1) == 0 and SAMPLE_G * ss <= SAMPLE_LK

    w_main = w_in[:, :, :MAIN_COLS].astype(_BF)
    gate_pad = LANES - 2 * N_HEADS
    w_gate = jnp.pad(w_in[:, :, MAIN_COLS:], ((0, 0), (0, 0), (0, gate_pad))).astype(_BF)
    b_main = b_in[:, None, :MAIN_COLS]
    b_gate = jnp.pad(b_in[:, MAIN_COLS:], ((0, 0), (0, gate_pad)))[:, None, :]
    w_out_b = w_out.astype(_BF)
    row = lambda t: t[:, None, :]
    hn_r, bdw_r, cg_r, cb_r, lg_r, lb_r = map(row, (hn_g, b_dw, cln_g, cln_b, ln_g, ln_b))
    m_lanes = jnp.broadcast_to(state_m[..., None], state_m.shape + (LANES,))
    zero_buf = jnp.zeros((1, bp, CONV_K - 1, C_WIDTH), _F32)

    yp = x_prompt.reshape(bp * sp, D_MODEL)
    ys = x_sample.reshape(bs * ss, D_MODEL)
    cp_l, np_l, mp_l, ns_l, ms_l = [], [], [], [], []
    conv_p = conv_s = c_s = None
    for l in range(DEPTH):
        proj_p, gates_p = _in_proj(yp, w_main, b_main, w_gate, b_gate, l)
        proj_s, gates_s = _in_proj(ys, w_main, b_main, w_gate, b_gate, l)

        mm_p, c_p, n_p, m_p = _mlstm_prompt(proj_p, gates_p, hn_r, l, bp, sp)
        mm_s, c_s, n_s, m_s = _mlstm_sample(proj_s, gates_s, hn_r, state_C, state_n, m_lanes, c_s, l, bs, ss)
        mc_p, conv_p = _conv_module(proj_p, zero_buf, 0, w_dw, bdw_r, cg_r, cb_r, conv_p, l, bp, sp, CONV_TT)
        mc_s, conv_s = _conv_module(proj_s, state_conv, l, w_dw, bdw_r, cg_r, cb_r, conv_s, l, bs, ss, ss)

        yp = _out_proj(mm_p, mc_p, yp, w_out_b, lg_r, lb_r, l)
        ys = _out_proj(mm_s, mc_s, ys, w_out_b, lg_r, lb_r, l)
        cp_l.append(c_p); np_l.append(n_p); mp_l.append(m_p[..., 0])
        ns_l.append(n_s); ms_l.append(m_s[..., 0])

    return (yp.reshape(bp, sp, D_MODEL), ys.reshape(bs, ss, D_MODEL),
            jnp.stack(cp_l), jnp.stack(np_l), jnp.stack(mp_l), conv_p,
            c_s, jnp.stack(ns_l), jnp.stack(ms_l), conv_s)
```

```python
import functools

import jax
import jax.numpy as jnp
from jax import lax
from jax.experimental import pallas as pl
from jax.experimental.pallas import tpu as pltpu

D_MODEL = 2048
DEPTH = 2
M_WIDTH = D_MODEL // 2
C_WIDTH = D_MODEL - M_WIDTH
N_HEADS = 4
HEAD_DIM = M_WIDTH // N_HEADS
CONV_K = 31
LN_EPS = 1e-5
DN_ALPHA = (2 * DEPTH) ** 0.25
MAIN_COLS = 5 * M_WIDTH + 3 * C_WIDTH
K_SCALE = HEAD_DIM ** -0.5

LANES = 128
SUBLANES = 8
VMEM_LIMIT = 56 * 1024 * 1024

PROJ_TM = 1024
PROJ_TN = 1024
CHUNK = 256
SAMPLE_G = SUBLANES
SAMPLE_LK = 128
CONV_TT = 256
CONV_HALO = 32
CONV_STRIDE = 4
CONV_GROUP = CONV_STRIDE * SUBLANES
C_SLABS = C_WIDTH // LANES
CONV_GB = SUBLANES
OUT_TM = 256

_BF = jnp.bfloat16
_F32 = jnp.float32
_NT = (((1,), (1,)), ((), ()))


def _params(sem):
    return pltpu.CompilerParams(dimension_semantics=sem, vmem_limit_bytes=VMEM_LIMIT)


def _sigmoid(x):
    return 1.0 / (1.0 + jnp.exp(-x))


def _log_sigmoid(x):
    return jnp.minimum(x, 0.0) - jnp.log1p(jnp.exp(-jnp.abs(x)))


def _in_proj_kernel(x_ref, w_ref, b_ref, wg_ref, bg_ref, proj_ref, gates_ref, xb_ref):
    @pl.when(pl.program_id(1) == 0)
    def _():
        xb_ref[...] = x_ref[...].astype(_BF)
        gates_ref[...] = lax.dot_general(xb_ref[...], wg_ref[...], _NT, preferred_element_type=_F32) + bg_ref[...]

    proj_ref[...] = lax.dot_general(xb_ref[...], w_ref[...], _NT, preferred_element_type=_F32) + b_ref[...]


def _in_proj(x, w_t, b_main, w_gate_t, b_gate, layer):
    T = x.shape[0]
    tm = min(PROJ_TM, T)
    grid = (T // tm, MAIN_COLS // PROJ_TN)
    return pl.pallas_call(
        _in_proj_kernel,
        grid=grid,
        in_specs=[
            pl.BlockSpec((tm, D_MODEL), lambda i, j: (i, 0)),
            pl.BlockSpec((None, PROJ_TN, D_MODEL), lambda i, j: (layer, j, 0)),
            pl.BlockSpec((None, 1, PROJ_TN), lambda i, j: (layer, 0, j)),
            pl.BlockSpec((None, LANES, D_MODEL), lambda i, j: (layer, 0, 0)),
            pl.BlockSpec((None, 1, LANES), lambda i, j: (layer, 0, 0)),
        ],
        out_specs=[
            pl.BlockSpec((tm, PROJ_TN), lambda i, j: (i, j)),
            pl.BlockSpec((tm, LANES), lambda i, j: (i, 0)),
        ],
        out_shape=[
            jax.ShapeDtypeStruct((T, MAIN_COLS), _F32),
            jax.ShapeDtypeStruct((T, LANES), _F32),
        ],
        scratch_shapes=[pltpu.VMEM((tm, D_MODEL), _BF)],
        compiler_params=_params(("arbitrary", "arbitrary")),
        name="in_proj",
    )(x, w_t, b_main, w_gate_t, b_gate)


def _seg_scan(x, seg, stride, op, ident):
    time = (lax.broadcasted_iota(jnp.int32, x.shape, 0) // stride) & (seg - 1)
    s = 1
    while s < seg:
        shifted = pltpu.roll(x, s * stride, axis=0)
        x = op(x, jnp.where(time >= s, shifted, ident))
        s *= 2
    return x


def _pad_rows(x, rows):
    if x.shape[0] == rows:
        return x
    return jnp.concatenate([x, jnp.zeros((rows - x.shape[0], x.shape[1]), x.dtype)], axis=0)


def _mlstm_block(load, store, g, hn_ref, c_src, n_src, m_src, c_dst, n_dst, m_dst, *, seg, groups, lk):
    lq = seg * groups
    b_cum = _seg_scan(_log_sigmoid(g), seg, groups, jnp.add, 0.0)
    b_al = pltpu.roll(b_cum, LANES - N_HEADS, axis=1)
    a = g - b_al
    a_max = _seg_scan(a, seg, groups, jnp.maximum, -jnp.inf)
    a_t = _pad_rows(a, lk).T

    row = lax.broadcasted_iota(jnp.int32, (lq, lk), 0)
    col = lax.broadcasted_iota(jnp.int32, (lq, lk), 1)
    if groups == 1:
        mask = col <= row
    else:
        mask = ((col // groups) <= (row // groups)) & ((col & (groups - 1)) == (row & (groups - 1)))
    rid = lax.broadcasted_iota(jnp.int32, (lq, 1), 0) & (groups - 1)
    cid = lax.broadcasted_iota(jnp.int32, (1, lk), 1) & (groups - 1)
    last = [(seg - 1) * groups + s for s in range(groups)]

    def per_row(vals):
        out = vals[0]
        for s in range(1, groups):
            out = jnp.where(rid == s, vals[s], out)
        return out

    for h in range(N_HEADS):
        a_col = a[:, h:h + 1]
        b_col = b_al[:, h:h + 1]
        a_row = a_t[h:h + 1, :]
        m_prev = [m_src[s, h:h + 1, 0:1] for s in range(groups)]
        m_prev_rows = per_row(m_prev)
        mx = jnp.maximum(a_max[:, h:h + 1], m_prev_rows)
        m_t = b_col + mx
        w_inter = jnp.exp(m_prev_rows - mx)

        q = load("q", h)
        k = load("k", h) * K_SCALE
        qb = q.astype(_BF)
        kb = _pad_rows(k.astype(_BF), lk)
        vb = _pad_rows(load("v", h).astype(_BF), lk)

        s_mat = lax.dot_general(qb, kb, _NT, preferred_element_type=_F32)
        s_mat = s_mat * jnp.exp(jnp.where(mask, a_row - mx, -jnp.inf))
        num = jnp.dot(s_mat.astype(_BF), vb, preferred_element_type=_F32)
        if groups == 1:
            inter = jnp.dot(qb, c_src[0, h].astype(_BF), preferred_element_type=_F32)
            qn = jnp.sum(q * n_src[0, h:h + 1, :], axis=1, keepdims=True)
        else:
            inter = jnp.zeros((lq, HEAD_DIM), _F32)
            for s in range(groups):
                q_s = jnp.where(rid == s, qb, jnp.zeros_like(qb))
                inter = inter + jnp.dot(q_s, c_src[s, h].astype(_BF), preferred_element_type=_F32)
            qn = jnp.sum(q * per_row([n_src[s, h:h + 1, :] for s in range(groups)]), axis=1, keepdims=True)
        num = w_inter * inter + num
        den = w_inter * qn + jnp.sum(s_mat, axis=1, keepdims=True)
        hh = num / jnp.maximum(jnp.abs(den), jnp.exp(-m_t))

        mu = jnp.mean(hh, axis=1, keepdims=True)
        hc = hh - mu
        var = jnp.mean(hc * hc, axis=1, keepdims=True)
        hn = hc * lax.rsqrt(var + LN_EPS) * hn_ref[:, h * HEAD_DIM:(h + 1) * HEAD_DIM]
        z = load("z", h)
        store(h, hn * _sigmoid(load("o", h)) * (z * _sigmoid(z)))

        m_new = [m_t[r:r + 1, :] for r in last]
        b_end = [b_col[r:r + 1, :] for r in last]
        ws = jnp.exp(a_col + per_row([b_end[s] - m_new[s] for s in range(groups)]))
        kw = k * ws
        kw_t = _pad_rows(kw, lk).T
        for s in range(groups):
            decay = jnp.exp(b_end[s] + m_prev[s] - m_new[s])
            if groups == 1:
                kw_ts, kw_s = kw_t, kw
            else:
                kw_ts = jnp.where(cid == s, kw_t, 0.0)
                kw_s = jnp.where(rid == s, kw, 0.0)
            upd = jnp.dot(kw_ts.astype(_BF), vb, preferred_element_type=_F32)
            c_dst[s, h] = decay * c_src[s, h] + upd
            n_dst[s, h:h + 1, :] = decay * n_src[s, h:h + 1, :] + jnp.sum(kw_s, axis=0, keepdims=True)
            m_dst[s, h:h + 1, :] = jnp.broadcast_to(m_new[s], (1, LANES))


def _mlstm_prompt_kernel(q_ref, k_ref, v_ref, o_ref, z_ref, g_ref, hn_ref,
                         mix_ref, c_ref, n_ref, m_ref, *, chunk):
    @pl.when(pl.program_id(1) == 0)
    def _():
        c_ref[...] = jnp.zeros_like(c_ref)
        n_ref[...] = jnp.zeros_like(n_ref)
        m_ref[...] = jnp.zeros_like(m_ref)

    refs = dict(q=q_ref, k=k_ref, v=v_ref, o=o_ref, z=z_ref)

    def load(name, h):
        return refs[name][:, h * HEAD_DIM:(h + 1) * HEAD_DIM]

    def store(h, mix):
        mix_ref[:, h * HEAD_DIM:(h + 1) * HEAD_DIM] = mix.astype(mix_ref.dtype)

    _mlstm_block(load, store, g_ref[...], hn_ref, c_ref, n_ref, m_ref, c_ref, n_ref, m_ref,
                 seg=chunk, groups=1, lk=chunk)


def _mlstm_prompt(proj, gates, hn_g, layer, batch, seq):
    nc = seq // CHUNK
    col = lambda cb: pl.BlockSpec((CHUNK, M_WIDTH), lambda b, c: (b * nc + c, cb))
    return pl.pallas_call(
        functools.partial(_mlstm_prompt_kernel, chunk=CHUNK),
        grid=(batch, nc),
        in_specs=[col(0), col(1), col(2), col(3), col(4),
                  pl.BlockSpec((CHUNK, LANES), lambda b, c: (b * nc + c, 0)),
                  pl.BlockSpec((None, 1, M_WIDTH), lambda b, c: (layer, 0, 0))],
        out_specs=[
            pl.BlockSpec((CHUNK, M_WIDTH), lambda b, c: (b * nc + c, 0)),
            pl.BlockSpec((1, N_HEADS, HEAD_DIM, HEAD_DIM), lambda b, c: (b, 0, 0, 0)),
            pl.BlockSpec((1, N_HEADS, HEAD_DIM), lambda b, c: (b, 0, 0)),
            pl.BlockSpec((1, N_HEADS, LANES), lambda b, c: (b, 0, 0)),
        ],
        out_shape=[
            jax.ShapeDtypeStruct((batch * seq, M_WIDTH), _BF),
            jax.ShapeDtypeStruct((batch, N_HEADS, HEAD_DIM, HEAD_DIM), _F32),
            jax.ShapeDtypeStruct((batch, N_HEADS, HEAD_DIM), _F32),
            jax.ShapeDtypeStruct((batch, N_HEADS, LANES), _F32),
        ],
        compiler_params=_params(("arbitrary", "arbitrary")),
        name="mlstm_prompt",
    )(proj, proj, proj, proj, proj, gates, hn_g)


def _mlstm_sample_kernel(q_ref, k_ref, v_ref, o_ref, z_ref, g_ref, hn_ref, c_in, n_in, m_in, *rest, seg):
    mix_ref, c_out, n_out, m_out = rest[-4:]
    lq = seg * SAMPLE_G
    refs = dict(q=q_ref, k=k_ref, v=v_ref, o=o_ref, z=z_ref)

    def load(name, h):
        return refs[name][:, :, h * HEAD_DIM:(h + 1) * HEAD_DIM].reshape(lq, HEAD_DIM)

    def store(h, mix):
        mix_ref[:, :, h * HEAD_DIM:(h + 1) * HEAD_DIM] = mix.reshape(seg, SAMPLE_G, HEAD_DIM)

    _mlstm_block(load, store, g_ref[...].reshape(lq, LANES), hn_ref, c_in, n_in, m_in, c_out, n_out, m_out,
                 seg=seg, groups=SAMPLE_G, lk=SAMPLE_LK)


def _mlstm_sample(proj, gates, hn_g, state_c, state_n, m_lanes, c_all, layer):
    seq, batch, _ = proj.shape
    g = SAMPLE_G
    col = lambda cb: pl.BlockSpec((seq, g, M_WIDTH), lambda i: (0, i, cb))
    in_specs = [col(0), col(1), col(2), col(3), col(4),
                pl.BlockSpec((seq, g, LANES), lambda i: (0, i, 0)),
                pl.BlockSpec((None, 1, M_WIDTH), lambda i: (layer, 0, 0)),
                pl.BlockSpec((None, g, N_HEADS, HEAD_DIM, HEAD_DIM), lambda i: (layer, i, 0, 0, 0)),
                pl.BlockSpec((None, g, N_HEADS, HEAD_DIM), lambda i: (layer, i, 0, 0)),
                pl.BlockSpec((None, g, N_HEADS, LANES), lambda i: (layer, i, 0, 0))]
    args = [proj, proj, proj, proj, proj, gates, hn_g, state_c, state_n, m_lanes]
    aliases = {}
    if c_all is not None:
        in_specs.append(pl.BlockSpec(memory_space=pl.ANY))
        args.append(c_all)
        aliases = {len(args) - 1: 1}
    return pl.pallas_call(
        functools.partial(_mlstm_sample_kernel, seg=seq),
        grid=(batch // g,),
        in_specs=in_specs,
        out_specs=[
            pl.BlockSpec((seq, g, M_WIDTH), lambda i: (0, i, 0)),
            pl.BlockSpec((None, g, N_HEADS, HEAD_DIM, HEAD_DIM), lambda i: (layer, i, 0, 0, 0)),
            pl.BlockSpec((g, N_HEADS, HEAD_DIM), lambda i: (i, 0, 0)),
            pl.BlockSpec((g, N_HEADS, LANES), lambda i: (i, 0, 0)),
        ],
        out_shape=[
            jax.ShapeDtypeStruct((seq, batch, M_WIDTH), _F32),
            jax.ShapeDtypeStruct((DEPTH, batch, N_HEADS, HEAD_DIM, HEAD_DIM), _F32),
            jax.ShapeDtypeStruct((batch, N_HEADS, HEAD_DIM), _F32),
            jax.ShapeDtypeStruct((batch, N_HEADS, LANES), _F32),
        ],
        input_output_aliases=aliases,
        compiler_params=_params(("arbitrary",)),
        name="mlstm_sample",
    )(*args)


def _conv_finish(y, cg, cb, zc):
    mu = jnp.mean(y, axis=-1, keepdims=True)
    yc = y - mu
    var = jnp.mean(yc * yc, axis=-1, keepdims=True)
    yn = yc * lax.rsqrt(var + LN_EPS) * cg + cb
    return (yn * _sigmoid(yn)) * (zc * _sigmoid(zc))


def _conv_prompt_kernel(ga_ref, gg_ref, zc_ref, w_ref, bdw_ref, cg_ref, cb_ref,
                        mix_ref, st_ref, full_ref, y_ref, *, tt):
    first = CONV_HALO - (CONV_K - 1)

    @pl.when(pl.program_id(1) == 0)
    def _():
        full_ref[:, 0:CONV_HALO, :] = jnp.zeros((C_SLABS, CONV_HALO, LANES), _F32)

    @pl.when(pl.program_id(1) > 0)
    def _():
        full_ref[:, 0:CONV_HALO, :] = full_ref[:, tt:tt + CONV_HALO, :]

    u = ga_ref[...] * _sigmoid(gg_ref[...])
    for lb in range(C_SLABS):
        full_ref[lb, CONV_HALO:CONV_HALO + tt, :] = u[:, lb * LANES:(lb + 1) * LANES]

    for lb in range(C_SLABS):
        ls = slice(lb * LANES, (lb + 1) * LANES)
        taps = [jnp.broadcast_to(w_ref[j:j + 1, ls], (SUBLANES, LANES)) for j in range(CONV_K)]
        bias = jnp.broadcast_to(bdw_ref[:, ls], (SUBLANES, LANES))

        def group(gi, carry, lb=lb, taps=taps, bias=bias):
            base = gi * CONV_GROUP
            acc = [bias] * CONV_STRIDE
            for o in range(CONV_K + CONV_STRIDE - 1):
                x_o = full_ref[lb, pl.ds(base + first + o, SUBLANES, stride=CONV_STRIDE), :]
                for step in range(CONV_STRIDE):
                    j = o - step
                    if 0 <= j < CONV_K:
                        acc[step] = acc[step] + x_o * taps[j]
            for step in range(CONV_STRIDE):
                y_ref[lb, pl.ds(base + step, SUBLANES, stride=CONV_STRIDE), :] = acc[step]
            return carry

        lax.fori_loop(0, tt // CONV_GROUP, group, 0)

    y = jnp.concatenate([y_ref[lb] for lb in range(C_SLABS)], axis=1)
    mix_ref[...] = _conv_finish(y, cg_ref[...], cb_ref[...], zc_ref[...]).astype(mix_ref.dtype)
    for lb in range(C_SLABS):
        st_ref[:, lb * LANES:(lb + 1) * LANES] = full_ref[lb, tt + first:tt + CONV_HALO, :]


def _conv_prompt(proj, w_dw, b_dw, cln_g, cln_b, layer, batch, seq):
    tt = CONV_TT
    nt = seq // tt
    first_col = 5 * M_WIDTH // C_WIDTH
    col = lambda cb: pl.BlockSpec((tt, C_WIDTH), lambda b, t: (b * nt + t, first_col + cb))
    vec = pl.BlockSpec((None, 1, C_WIDTH), lambda b, t: (layer, 0, 0))
    return pl.pallas_call(
        functools.partial(_conv_prompt_kernel, tt=tt),
        grid=(batch, nt),
        in_specs=[col(0), col(1), col(2),
                  pl.BlockSpec((None, CONV_K, C_WIDTH), lambda b, t: (layer, 0, 0)),
                  vec, vec, vec],
        out_specs=[
            pl.BlockSpec((tt, C_WIDTH), lambda b, t: (b * nt + t, 0)),
            pl.BlockSpec((None, CONV_K - 1, C_WIDTH), lambda b, t: (b, 0, 0)),
        ],
        out_shape=[
            jax.ShapeDtypeStruct((batch * seq, C_WIDTH), _BF),
            jax.ShapeDtypeStruct((batch, CONV_K - 1, C_WIDTH), _F32),
        ],
        scratch_shapes=[pltpu.VMEM((C_SLABS, CONV_HALO + tt, LANES), _F32),
                        pltpu.VMEM((C_SLABS, tt, LANES), _F32)],
        compiler_params=_params(("arbitrary", "arbitrary")),
        name="conv_prompt",
    )(proj, proj, proj, w_dw, b_dw, cln_g, cln_b)


def _conv_sample_kernel(ga_ref, gg_ref, zc_ref, buf_ref, w_ref, bdw_ref, cg_ref, cb_ref, *rest, seq):
    mix_ref, st_ref = rest[-2:]
    hist = CONV_K - 1
    u = [ga_ref[t] * _sigmoid(gg_ref[t]) for t in range(seq)]

    def full(r):
        return buf_ref[r] if r < hist else u[r - hist]

    taps = [w_ref[j:j + 1, :] for j in range(CONV_K)]
    for t in range(seq):
        acc = jnp.broadcast_to(bdw_ref[...], (CONV_GB, C_WIDTH))
        for j in range(CONV_K):
            acc = acc + full(t + j) * taps[j]
        mix_ref[t] = _conv_finish(acc, cg_ref[...], cb_ref[...], zc_ref[t])
    for r in range(hist):
        st_ref[r] = full(seq + r)


def _conv_sample(proj, buf_t, w_dw, b_dw, cln_g, cln_b, st_all, layer):
    seq, batch, _ = proj.shape
    gb = CONV_GB
    first_col = 5 * M_WIDTH // C_WIDTH
    col = lambda cb: pl.BlockSpec((seq, gb, C_WIDTH), lambda i: (0, i, first_col + cb))
    vec = pl.BlockSpec((None, 1, C_WIDTH), lambda i: (layer, 0, 0))
    in_specs = [col(0), col(1), col(2),
                pl.BlockSpec((None, CONV_K - 1, gb, C_WIDTH), lambda i: (layer, 0, i, 0)),
                pl.BlockSpec((None, CONV_K, C_WIDTH), lambda i: (layer, 0, 0)),
                vec, vec, vec]
    args = [proj, proj, proj, buf_t, w_dw, b_dw, cln_g, cln_b]
    aliases = {}
    if st_all is not None:
        in_specs.append(pl.BlockSpec(memory_space=pl.ANY))
        args.append(st_all)
        aliases = {len(args) - 1: 1}
    return pl.pallas_call(
        functools.partial(_conv_sample_kernel, seq=seq),
        grid=(batch // gb,),
        in_specs=in_specs,
        out_specs=[
            pl.BlockSpec((seq, gb, C_WIDTH), lambda i: (0, i, 0)),
            pl.BlockSpec((None, CONV_K - 1, gb, C_WIDTH), lambda i: (layer, 0, i, 0)),
        ],
        out_shape=[
            jax.ShapeDtypeStruct((seq, batch, C_WIDTH), _F32),
            jax.ShapeDtypeStruct((DEPTH, CONV_K - 1, batch, C_WIDTH), _F32),
        ],
        input_output_aliases=aliases,
        compiler_params=_params(("arbitrary",)),
        name="conv_sample",
    )(*args)


def _out_proj_kernel(mm_ref, mc_ref, x_ref, w_ref, g_ref, b_ref, y_ref):
    acc = jnp.dot(mm_ref[...].astype(_BF), w_ref[0:M_WIDTH, :], preferred_element_type=_F32)
    acc = acc + jnp.dot(mc_ref[...].astype(_BF), w_ref[M_WIDTH:D_MODEL, :], preferred_element_type=_F32)
    z = DN_ALPHA * x_ref[...] + acc
    mu = jnp.mean(z, axis=1, keepdims=True)
    zc = z - mu
    var = jnp.mean(zc * zc, axis=1, keepdims=True)
    y_ref[...] = zc * lax.rsqrt(var + LN_EPS) * g_ref[...] + b_ref[...]


def _out_proj(mix_m, mix_c, x, w_out, ln_g, ln_b, layer):
    T = x.shape[0]
    tm = min(OUT_TM, T)
    vec = pl.BlockSpec((None, 1, D_MODEL), lambda i: (layer, 0, 0))
    return pl.pallas_call(
        _out_proj_kernel,
        grid=(T // tm,),
        in_specs=[
            pl.BlockSpec((tm, M_WIDTH), lambda i: (i, 0)),
            pl.BlockSpec((tm, C_WIDTH), lambda i: (i, 0)),
            pl.BlockSpec((tm, D_MODEL), lambda i: (i, 0)),
            pl.BlockSpec((None, D_MODEL, D_MODEL), lambda i: (layer, 0, 0)),
            vec, vec,
        ],
        out_specs=pl.BlockSpec((tm, D_MODEL), lambda i: (i, 0)),
        out_shape=jax.ShapeDtypeStruct((T, D_MODEL), _F32),
        compiler_params=_params(("arbitrary",)),
        name="out_proj",
    )(mix_m, mix_c, x, w_out, ln_g, ln_b)


def kernel(x_prompt, x_sample, state_C, state_n, state_m, state_conv,
           w_in, b_in, hn_g, w_dw, b_dw, cln_g, cln_b, w_out, ln_g, ln_b):
    bp, sp, _ = x_prompt.shape
    bs, ss, _ = x_sample.shape
    assert sp % CHUNK == 0 and sp % CONV_TT == 0 and CONV_TT % CONV_GROUP == 0
    assert bs % SAMPLE_G == 0 and bs % CONV_GB == 0 and ss & (ss - 1) == 0 and SAMPLE_G * ss <= SAMPLE_LK

    w_t = jnp.swapaxes(w_in, 1, 2)
    gate_pad = LANES - 2 * N_HEADS
    w_gate_t = jnp.pad(w_t[:, MAIN_COLS:, :], ((0, 0), (0, gate_pad), (0, 0))).astype(_BF)
    w_t = w_t.astype(_BF)
    b_main = b_in[:, None, :MAIN_COLS]
    b_gate = jnp.pad(b_in[:, MAIN_COLS:], ((0, 0), (0, gate_pad)))[:, None, :]
    w_out_b = w_out.astype(_BF)
    row = lambda t: t[:, None, :]
    hn_r, bdw_r, cg_r, cb_r, lg_r, lb_r = map(row, (hn_g, b_dw, cln_g, cln_b, ln_g, ln_b))
    m_lanes = jnp.broadcast_to(state_m[..., None], state_m.shape + (LANES,))
    buf_t = jnp.swapaxes(state_conv, 1, 2)

    yp = x_prompt.reshape(bp * sp, D_MODEL)
    ys = jnp.swapaxes(x_sample, 0, 1).reshape(ss * bs, D_MODEL)
    cp_l, np_l, mp_l, stp_l, ns_l, ms_l = [], [], [], [], [], []
    conv_s = c_s = None
    for l in range(DEPTH):
        proj_p, gates_p = _in_proj(yp, w_t, b_main, w_gate_t, b_gate, l)
        proj_s, gates_s = _in_proj(ys, w_t, b_main, w_gate_t, b_gate, l)
        proj_s = proj_s.reshape(ss, bs, MAIN_COLS)
        gates_s = gates_s.reshape(ss, bs, LANES)

        mm_p, c_p, n_p, m_p = _mlstm_prompt(proj_p, gates_p, hn_r, l, bp, sp)
        mm_s, c_s, n_s, m_s = _mlstm_sample(proj_s, gates_s, hn_r, state_C, state_n, m_lanes, c_s, l)
        mc_p, st_p = _conv_prompt(proj_p, w_dw, bdw_r, cg_r, cb_r, l, bp, sp)
        mc_s, conv_s = _conv_sample(proj_s, buf_t, w_dw, bdw_r, cg_r, cb_r, conv_s, l)

        yp = _out_proj(mm_p, mc_p, yp, w_out_b, lg_r, lb_r, l)
        ys = _out_proj(mm_s.reshape(ss * bs, M_WIDTH), mc_s.reshape(ss * bs, C_WIDTH), ys, w_out_b, lg_r, lb_r, l)
        cp_l.append(c_p); np_l.append(n_p); mp_l.append(m_p[..., 0]); stp_l.append(st_p)
        ns_l.append(n_s); ms_l.append(m_s[..., 0])

    return (yp.reshape(bp, sp, D_MODEL), jnp.swapaxes(ys.reshape(ss, bs, D_MODEL), 0, 1),
            jnp.stack(cp_l), jnp.stack(np_l), jnp.stack(mp_l), jnp.stack(stp_l),
            c_s, jnp.stack(ns_l), jnp.stack(ms_l), jnp.swapaxes(conv_s, 1, 2))
```
